```python
import jax, jax.numpy as jnp
from jax import lax
import numpy as np

D_MODEL = 1024
BATCH = 8
SEQ = 2048
DEPTH = 4

EPS_RMS = 1e-6
EPS_LN = 1e-5
W_CONV = D_MODEL
CONV_K = 31
W_POOL = D_MODEL
POOL_WINDOWS = (2, 4, 8, 16)
POOL_GROUPS = len(POOL_WINDOWS)
POOL_GW = W_POOL // POOL_GROUPS
W_EVEN_IN = 3 * W_CONV + 2 * W_POOL
W_EVEN_MIX = W_CONV + W_POOL
LRU_HEADS = 12
LRU_HD = 128
W_LRU = LRU_HEADS * LRU_HD
LRU_CONV_K = 4
LRU_C = 8.0
N_EVEN = (DEPTH + 1) // 2
N_ODD = DEPTH // 2

kernel_name = "hybrid_conv_pool_rglru_trunk"


def rmsnorm(x, g):
    xf = x.astype(jnp.float32)
    y = xf * lax.rsqrt(jnp.mean(xf * xf, axis=-1, keepdims=True) + EPS_RMS)
    return (y * g.astype(jnp.float32)).astype(x.dtype)


def layernorm(x, g, b):
    xf = x.astype(jnp.float32)
    mu = jnp.mean(xf, axis=-1, keepdims=True)
    var = jnp.mean(jnp.square(xf - mu), axis=-1, keepdims=True)
    y = (xf - mu) * lax.rsqrt(var + EPS_LN)
    return (y * g.astype(jnp.float32) + b.astype(jnp.float32)).astype(x.dtype)


def causal_depthwise_conv(x, w, b):
    k = w.shape[0]
    y = lax.conv_general_dilated(
        x, w[:, None, :].astype(x.dtype), window_strides=(1,), padding=[(k - 1, 0)],
        dimension_numbers=("NWC", "WIO", "NWC"), feature_group_count=x.shape[-1])
    return y + b.astype(x.dtype)


def multiscale_pool_diff(v):
    bsz, t, _ = v.shape
    vf = v.astype(jnp.float32)
    cs_pad = jnp.pad(jnp.cumsum(vf, axis=1), ((0, 0), (1, 0), (0, 0)))
    pos = jnp.arange(1, t + 1, dtype=jnp.float32)
    outs = []
    for g, w in enumerate(POOL_WINDOWS):
        seg = cs_pad[:, :, g * POOL_GW:(g + 1) * POOL_GW]
        upper = seg[:, 1:]
        lower = jnp.pad(seg[:, :t - w + 1], ((0, 0), (w - 1, 0), (0, 0)))
        cnt = jnp.minimum(pos, jnp.float32(w))[None, :, None]
        outs.append((upper - lower) / cnt)
    return jnp.concatenate(outs, axis=-1) - vf


def even_mixer(h, w_in, conv_w, conv_b, ln_g, ln_b, pool_w, pool_b, pool_scale, w_out):
    bsz, t, _ = h.shape
    p = jnp.einsum("btd,dn->btn", h, w_in)
    a_val, a_glu, a_gate, b_val, b_gate = jnp.split(
        p, [W_CONV, 2 * W_CONV, 3 * W_CONV, 3 * W_CONV + W_POOL], axis=-1)
    u = a_val * jax.nn.sigmoid(a_glu)
    u = causal_depthwise_conv(u, conv_w, conv_b)
    u = jax.nn.silu(layernorm(u, ln_g, ln_b))
    ya = u * jax.nn.silu(a_gate)
    d = multiscale_pool_diff(b_val).reshape(bsz, t, POOL_GROUPS, POOL_GW)
    d = jnp.einsum("btgc,gce->btge", d, pool_w.astype(jnp.float32)) + pool_b.astype(jnp.float32)
    yb = (d.reshape(bsz, t, W_POOL) * pool_scale.astype(jnp.float32)).astype(h.dtype)
    yb = yb * jax.nn.silu(b_gate)
    y = jnp.concatenate([ya, yb], axis=-1)
    return jnp.einsum("btn,nd->btd", y, w_out)


def _lin_combine(c1, c2):
    a1, b1 = c1
    a2, b2 = c2
    return a1 * a2, a2 * b1 + b2


def odd_mixer(h, w_in, conv_w, conv_b, w_rg, b_rg, w_ig, b_ig, lam, w_out):
    bsz, t, _ = h.shape
    p = jnp.einsum("btd,dn->btn", h, w_in)
    xr, gate = jnp.split(p, [W_LRU], axis=-1)
    xc = causal_depthwise_conv(xr, conv_w, conv_b)
    xh = xc.reshape(bsz, t, LRU_HEADS, LRU_HD)
    r = jax.nn.sigmoid(jnp.einsum("bthi,hij->bthj", xh, w_rg).reshape(bsz, t, W_LRU).astype(jnp.float32)
                       + b_rg.astype(jnp.float32))
    i = jax.nn.sigmoid(jnp.einsum("bthi,hij->bthj", xh, w_ig).reshape(bsz, t, W_LRU).astype(jnp.float32)
                       + b_ig.astype(jnp.float32))
    log_a = -LRU_C * r * jax.nn.softplus(-lam.astype(jnp.float32))
    a = jnp.exp(log_a)
    mult = jnp.sqrt(-jnp.expm1(2.0 * log_a))
    bterm = mult * (i * xc.astype(jnp.float32))
    _, hs = lax.associative_scan(_lin_combine, (a, bterm), axis=1)
    y = hs.astype(h.dtype) * jax.nn.silu(gate)
    return jnp.einsum("btn,nd->btd", y, w_out)


def setup_inputs(seed: int = 0) -> dict:
    key = jax.random.key(seed)
    ks = iter(jax.random.split(key, 32))
    f32 = jnp.float32

    def nrm(shape, scale):
        return jax.random.normal(next(ks), shape, f32) * scale

    x = jax.random.normal(next(ks), (BATCH, SEQ, D_MODEL), f32)
    a0 = jax.random.uniform(next(ks), (N_ODD, W_LRU), f32, 0.9, 0.999)
    s = a0 ** (1.0 / LRU_C)
    lru_lambda = jnp.log(s) - jnp.log1p(-s)
    return {
        "x": x,
        "norm_even": 1.0 + nrm((N_EVEN, D_MODEL), 0.02),
        "w_in_even": nrm((N_EVEN, D_MODEL, W_EVEN_IN), D_MODEL ** -0.5),
        "conv_a_w": nrm((N_EVEN, CONV_K, W_CONV), CONV_K ** -0.5),
        "conv_a_b": nrm((N_EVEN, W_CONV), 0.01),
        "ln_a_g": 1.0 + nrm((N_EVEN, W_CONV), 0.02),
        "ln_a_b": nrm((N_EVEN, W_CONV), 0.01),
        "pool_w": nrm((N_EVEN, POOL_GROUPS, POOL_GW, POOL_GW), POOL_GW ** -0.5),
        "pool_b": nrm((N_EVEN, POOL_GROUPS, POOL_GW), 0.01),
        "pool_scale": 1.0 + nrm((N_EVEN, W_POOL), 0.1),
        "w_out_even": nrm((N_EVEN, W_EVEN_MIX, D_MODEL), W_EVEN_MIX ** -0.5),
        "norm_odd": 1.0 + nrm((N_ODD, D_MODEL), 0.02),
        "w_in_odd": nrm((N_ODD, D_MODEL, 2 * W_LRU), D_MODEL ** -0.5),
        "conv_c_w": nrm((N_ODD, LRU_CONV_K, W_LRU), LRU_CONV_K ** -0.5),
        "conv_c_b": nrm((N_ODD, W_LRU), 0.01),
        "w_rg": nrm((N_ODD, LRU_HEADS, LRU_HD, LRU_HD), LRU_HD ** -0.5),
        "b_rg": nrm((N_ODD, W_LRU), 0.01),
        "w_ig": nrm((N_ODD, LRU_HEADS, LRU_HD, LRU_HD), LRU_HD ** -0.5),
        "b_ig": nrm((N_ODD, W_LRU), 0.01),
        "lru_lambda": lru_lambda,
        "w_out_odd": nrm((N_ODD, W_LRU, D_MODEL), W_LRU ** -0.5),
        "final_norm": 1.0 + nrm((D_MODEL,), 0.02),
    }


def reference(x, norm_even, w_in_even, conv_a_w, conv_a_b, ln_a_g, ln_a_b, pool_w, pool_b,
              pool_scale, w_out_even, norm_odd, w_in_odd, conv_c_w, conv_c_b, w_rg, b_rg,
              w_ig, b_ig, lru_lambda, w_out_odd, final_norm):
    h = x
    for layer in range(DEPTH):
        if layer % 2 == 0:
            j = layer // 2
            h = h + even_mixer(rmsnorm(h, norm_even[j]), w_in_even[j], conv_a_w[j], conv_a_b[j],
                               ln_a_g[j], ln_a_b[j], pool_w[j], pool_b[j], pool_scale[j],
                               w_out_even[j])
        else:
            j = layer // 2
            h = h + odd_mixer(rmsnorm(h, norm_odd[j]), w_in_odd[j], conv_c_w[j], conv_c_b[j],
                              w_rg[j], b_rg[j], w_ig[j], b_ig[j], lru_lambda[j], w_out_odd[j])
    return rmsnorm(h, final_norm)
```

```python
import functools

import jax
import jax.numpy as jnp
from jax import lax
from jax.experimental import pallas as pl
from jax.experimental.pallas import tpu as pltpu

EPS_RMS = 1e-6
EPS_LN = 1e-5
CONV_K = 31
POOL_WINDOWS = (2, 4, 8, 16)
POOL_GW = 256
LRU_CONV_K = 4
LRU_C = 8.0

LANE = 128
SUBLANE = 8
TT = 512
CONV_HALO = 32
POOL_HALO = 16
LRU_HALO = 8
CONV_ROWS = 32
VMEM_LIMIT_BYTES = 56 * 1024 * 1024


def _rmsnorm(x, g):
    ms = jnp.mean(x * x, axis=-1, keepdims=True)
    return x * lax.rsqrt(ms + EPS_RMS) * g


def _silu(x):
    return x * jax.nn.sigmoid(x)


def _dot(a, b):
    return jnp.dot(a, b, preferred_element_type=jnp.float32)


def _shifted(buf, chunk, start, rows):
    return buf[chunk, pl.ds(start, rows, stride=1), :]


def _carry_halo(buf, halo, first):
    n = buf.shape[0]

    @pl.when(first)
    def _():
        buf[:, 0:halo, :] = jnp.zeros((n, halo, LANE), jnp.float32)

    @pl.when(jnp.logical_not(first))
    def _():
        buf[:, 0:halo, :] = buf[:, TT:TT + halo, :]


def _even_kernel(h_ref, g_ref, win_ref, cw_ref, cb_ref, lng_ref, lnb_ref, pw_ref,
                 pb_ref, ps_ref, wout_ref, o_ref, ubuf, vbuf, cbuf):
    t = pl.program_id(1)
    first = t == 0
    d = h_ref.shape[-1]
    nch = d // LANE
    _carry_halo(ubuf, CONV_HALO, first)
    _carry_halo(vbuf, POOL_HALO, first)

    x = h_ref[0]
    hn = _rmsnorm(x, g_ref[...]).astype(jnp.bfloat16)

    u = _dot(hn, win_ref[:, 0:d]) * jax.nn.sigmoid(_dot(hn, win_ref[:, d:2 * d]))
    for c in range(nch):
        ubuf[c, CONV_HALO:CONV_HALO + TT, :] = u[:, c * LANE:(c + 1) * LANE]

    def conv_step(i, carry):
        r0 = pl.multiple_of(i * CONV_ROWS, CONV_ROWS)
        for c in range(nch):
            ls = slice(c * LANE, (c + 1) * LANE)
            acc = jnp.broadcast_to(cb_ref[:, ls], (CONV_ROWS, LANE))
            for k in range(CONV_K):
                acc = acc + cw_ref[k:k + 1, ls] * _shifted(
                    ubuf, c, r0 + (CONV_HALO - (CONV_K - 1)) + k, CONV_ROWS)
            cbuf[pl.ds(r0, CONV_ROWS), ls] = acc
        return carry

    lax.fori_loop(0, TT // CONV_ROWS, conv_step, 0)

    cv = cbuf[...]
    mu = jnp.mean(cv, axis=-1, keepdims=True)
    xc = cv - mu
    var = jnp.mean(xc * xc, axis=-1, keepdims=True)
    ln = xc * lax.rsqrt(var + EPS_LN) * lng_ref[...] + lnb_ref[...]
    ya = (_silu(ln) * _silu(_dot(hn, win_ref[:, 2 * d:3 * d]))).astype(jnp.bfloat16)

    bv = _dot(hn, win_ref[:, 3 * d:4 * d])
    for c in range(nch):
        vbuf[c, POOL_HALO:POOL_HALO + TT, :] = bv[:, c * LANE:(c + 1) * LANE]
    pos = t * TT + 1 + lax.broadcasted_iota(jnp.int32, (TT, LANE), 0)
    mixed = []
    for g, w in enumerate(POOL_WINDOWS):
        inv_cnt = 1.0 / jnp.minimum(pos, w).astype(jnp.float32)
        parts = []
        for c in range(g * POOL_GW // LANE, (g + 1) * POOL_GW // LANE):
            s = _shifted(vbuf, c, POOL_HALO, TT)
            for j in range(1, w):
                s = s + _shifted(vbuf, c, POOL_HALO - j, TT)
            parts.append(s * inv_cnt - _shifted(vbuf, c, POOL_HALO, TT))
        dg = jnp.concatenate(parts, axis=1).astype(jnp.bfloat16)
        mixed.append(_dot(dg, pw_ref[g]))
    pm = (jnp.concatenate(mixed, axis=1) + pb_ref[...]) * ps_ref[...]
    yb = (pm * _silu(_dot(hn, win_ref[:, 4 * d:5 * d]))).astype(jnp.bfloat16)

    o_ref[0] = x + _dot(ya, wout_ref[0:d, :]) + _dot(yb, wout_ref[d:2 * d, :])


def _odd_kernel(h_ref, g_ref, win_ref, cw_ref, cb_ref, wg_ref, brg_ref, big_ref,
                lam_ref, wout_ref, fin_ref, o_ref, xbuf, abuf, bbuf, hbuf, hcar,
                *, final_norm):
    t = pl.program_id(1)
    first = t == 0
    d = h_ref.shape[-1]
    wl = lam_ref.shape[-1]
    nch = wl // LANE
    ngrp = TT // SUBLANE
    _carry_halo(xbuf, LRU_HALO, first)

    @pl.when(first)
    def _():
        hcar[...] = jnp.zeros(hcar.shape, jnp.float32)

    x = h_ref[0]
    hn = _rmsnorm(x, g_ref[...]).astype(jnp.bfloat16)
    xr = _dot(hn, win_ref[:, 0:wl])
    for c in range(nch):
        xbuf[c, LRU_HALO:LRU_HALO + TT, :] = xr[:, c * LANE:(c + 1) * LANE]

    z = -lam_ref[...]
    decay = -LRU_C * (jnp.maximum(z, 0.0) + jnp.log1p(jnp.exp(-jnp.abs(z))))
    row = lax.broadcasted_iota(jnp.int32, (ngrp, SUBLANE, LANE), 1)

    for c in range(nch):
        ls = slice(c * LANE, (c + 1) * LANE)
        xc = jnp.broadcast_to(cb_ref[:, ls], (TT, LANE))
        for k in range(LRU_CONV_K):
            xc = xc + cw_ref[k:k + 1, ls] * _shifted(
                xbuf, c, LRU_HALO - (LRU_CONV_K - 1) + k, TT)
        pre = _dot(xc.astype(jnp.bfloat16), wg_ref[c])
        r = jax.nn.sigmoid(pre[:, 0:LANE] + brg_ref[:, ls])
        gi = jax.nn.sigmoid(pre[:, LANE:2 * LANE] + big_ref[:, ls])
        log_a = decay[:, ls] * r
        a = jnp.exp(log_a)
        b = jnp.sqrt(1.0 - a * a) * (gi * xc)
        a3 = a.reshape(ngrp, SUBLANE, LANE)
        b3 = b.reshape(ngrp, SUBLANE, LANE)
        for s in (1, 2, 4):
            keep = row >= s
            a_prev = jnp.where(keep, pltpu.roll(a3, s, axis=1), 1.0)
            b_prev = jnp.where(keep, pltpu.roll(b3, s, axis=1), 0.0)
            b3 = a3 * b_prev + b3
            a3 = a3 * a_prev
        abuf[c] = a3.reshape(TT, LANE)
        bbuf[c] = b3.reshape(TT, LANE)

    def group_step(g, hprev):
        r0 = pl.multiple_of(g * SUBLANE, SUBLANE)
        nxt = []
        for c in range(nch):
            hg = abuf[c, pl.ds(r0, SUBLANE), :] * hprev[c] + bbuf[c, pl.ds(r0, SUBLANE), :]
            hbuf[pl.ds(r0, SUBLANE), c * LANE:(c + 1) * LANE] = hg
            nxt.append(jnp.broadcast_to(hg[SUBLANE - 1:SUBLANE, :], (SUBLANE, LANE)))
        return tuple(nxt)

    hlast = lax.fori_loop(0, ngrp, group_step,
                          tuple(hcar[c] for c in range(nch)))
    for c in range(nch):
        hcar[c] = hlast[c]

    y = (hbuf[...] * _silu(_dot(hn, win_ref[:, wl:2 * wl]))).astype(jnp.bfloat16)
    out = x + _dot(y, wout_ref[...])
    if final_norm:
        out = _rmsnorm(out, fin_ref[...])
    o_ref[0] = out


def _const_spec(shape):
    zeros = (0,) * len(shape)
    return pl.BlockSpec(shape, lambda b, t: zeros, pipeline_mode=pl.Buffered(1))


def _row(v):
    return v.reshape(1, -1)


def _layer_call(body, h, consts, scratch, name):
    bsz, t, d = h.shape
    tile = pl.BlockSpec((1, TT, d), lambda b, i: (b, i, 0))
    return pl.pallas_call(
        body,
        grid=(bsz, t // TT),
        in_specs=[tile] + [_const_spec(c.shape) for c in consts],
        out_specs=tile,
        out_shape=jax.ShapeDtypeStruct(h.shape, h.dtype),
        scratch_shapes=scratch,
        compiler_params=pltpu.CompilerParams(
            dimension_semantics=("arbitrary", "arbitrary"),
            vmem_limit_bytes=VMEM_LIMIT_BYTES),
        name=name,
    )(h, *consts)


def _even_layer(h, norm, w_in, conv_w, conv_b, ln_g, ln_b, pool_w, pool_b, pool_scale, w_out):
    d = h.shape[-1]
    nch = d // LANE
    bf = jnp.bfloat16
    consts = [_row(norm), w_in.astype(bf), conv_w, _row(conv_b), _row(ln_g), _row(ln_b),
              pool_w.astype(bf), _row(pool_b), _row(pool_scale), w_out.astype(bf)]
    scratch = [pltpu.VMEM((nch, CONV_HALO + TT, LANE), jnp.float32),
               pltpu.VMEM((nch, POOL_HALO + TT, LANE), jnp.float32),
               pltpu.VMEM((TT, d), jnp.float32)]
    return _layer_call(_even_kernel, h, consts, scratch, "even_layer")


def _odd_layer(h, norm, w_in, conv_w, conv_b, w_rg, b_rg, w_ig, b_ig, lam, w_out,
               final_norm):
    bf = jnp.bfloat16
    wl = lam.shape[-1]
    nch = wl // LANE
    w_gate = jnp.concatenate([w_rg, w_ig], axis=-1).astype(bf)
    fin = final_norm if final_norm is not None else jnp.ones_like(norm)
    consts = [_row(norm), w_in.astype(bf), conv_w, _row(conv_b), w_gate, _row(b_rg),
              _row(b_ig), _row(lam), w_out.astype(bf), _row(fin)]
    scratch = [pltpu.VMEM((nch, LRU_HALO + TT, LANE), jnp.float32),
               pltpu.VMEM((nch, TT, LANE), jnp.float32),
               pltpu.VMEM((nch, TT, LANE), jnp.float32),
               pltpu.VMEM((TT, wl), jnp.float32),
               pltpu.VMEM((nch, SUBLANE, LANE), jnp.float32)]
    body = functools.partial(_odd_kernel, final_norm=final_norm is not None)
    return _layer_call(body, h, consts, scratch, "odd_layer")


def kernel(x, norm_even, w_in_even, conv_a_w, conv_a_b, ln_a_g, ln_a_b, pool_w, pool_b, pool_scale, w_out_even, norm_odd, w_in_odd, conv_c_w, conv_c_b, w_rg, b_rg, w_ig, b_ig, lru_lambda, w_out_odd, final_norm):
    depth = norm_even.shape[0] + norm_odd.shape[0]
    h = x
    for layer in range(depth):
        j = layer // 2
        if layer % 2 == 0:
            h = _even_layer(h, norm_even[j], w_in_even[j], conv_a_w[j], conv_a_b[j],
                            ln_a_g[j], ln_a_b[j], pool_w[j], pool_b[j], pool_scale[j],
                            w_out_even[j])
        else:
            h = _odd_layer(h, norm_odd[j], w_in_odd[j], conv_c_w[j], conv_c_b[j],
                           w_rg[j], b_rg[j], w_ig[j], b_ig[j], lru_lambda[j],
                           w_out_odd[j], final_norm if layer == depth - 1 else None)
    return h
```
